```python
import math
import jax, jax.numpy as jnp
from jax import lax
import numpy as np


D_MODEL = 2048
BATCH = 4
SEQ = 4096
DEPTH = 4
DEC_BATCH = 32
DEC_SEQ = 32
PAST_LEN = 1024

CHUNK = 64
N_A_LAYERS = DEPTH // 2
N_B_LAYERS = DEPTH - N_A_LAYERS
RET_HEADS = 8
RET_DK = D_MODEL // RET_HEADS
RET_DV = 2 * D_MODEL // RET_HEADS
DIFF_HEADS = 8
DIFF_HD = D_MODEL // (2 * DIFF_HEADS)
D_FF = 4 * D_MODEL
Q_BLOCK = 128
EPS = 1e-6

kernel_name = 'yoco_retention_diffattn_stream'


def rmsnorm(x, g=None):
    xf = x.astype(jnp.float32)
    y = xf * lax.rsqrt(jnp.mean(xf * xf, axis=-1, keepdims=True) + EPS)
    if g is not None:
        y = y * g.astype(jnp.float32)
    return y.astype(x.dtype)


def retention_scan(q, k, v, s0):
    B, T = q.shape[:2]
    C = min(CHUNK, T)
    n = T // C
    lg = jnp.log1p(-jnp.exp2(-5.0 - jnp.arange(RET_HEADS, dtype=jnp.float32)))
    i = jnp.arange(C, dtype=jnp.float32)
    dif = i[:, None] - i[None, :]
    decay_in = jnp.where(dif >= 0, jnp.exp(lg[:, None, None] * jnp.maximum(dif, 0.0)), 0.0)
    q_dec = jnp.exp(lg[None, :] * (i[:, None] + 1.0))
    k_dec = jnp.exp(lg[None, :] * (C - 1.0 - i[:, None]))
    c_dec = jnp.exp(lg * C)

    def to_chunks(a):
        return jnp.moveaxis(a.astype(jnp.float32).reshape(B, n, C, *a.shape[2:]), 1, 0)

    def step(S, qkv):
        qc, kc, vc = qkv
        sc = jnp.einsum('bihd,bjhd->bhij', qc, kc) * decay_in
        o = (jnp.einsum('bhij,bjhe->bihe', sc, vc)
             + jnp.einsum('bihd,bhde->bihe', qc * q_dec[None, :, :, None], S))
        S = S * c_dec[None, :, None, None] + jnp.einsum('bjhd,bjhe->bhde', kc * k_dec[None, :, :, None], vc)
        return S, o

    S, o = lax.scan(step, s0.astype(jnp.float32), (to_chunks(q), to_chunks(k), to_chunks(v)))
    o = jnp.moveaxis(o, 0, 1).reshape(B, T, RET_HEADS, RET_DV)
    return o.astype(v.dtype), S.astype(s0.dtype)


def retention_mixer(h, s0, w_in, w_out):
    B, T, _ = h.shape
    proj = h @ w_in
    q, k, v, g = jnp.split(proj, [D_MODEL, 2 * D_MODEL, 4 * D_MODEL], axis=-1)
    q = q.reshape(B, T, RET_HEADS, RET_DK)
    k = k.reshape(B, T, RET_HEADS, RET_DK) * (RET_DK ** -0.5)
    v = v.reshape(B, T, RET_HEADS, RET_DV)
    o, s_new = retention_scan(q, k, v, s0)
    o = rmsnorm(o).reshape(B, T, RET_HEADS * RET_DV)
    return (jax.nn.silu(g) * o) @ w_out, s_new


def shared_kv(x, g_kv, w_kv):
    B, T, _ = x.shape
    k, v = jnp.split(rmsnorm(x, g_kv) @ w_kv, [D_MODEL], axis=-1)
    return (k.reshape(B, T, DIFF_HEADS, 2, DIFF_HD), v.reshape(B, T, DIFF_HEADS, 2 * DIFF_HD))


def diff_attention(q, k, v, q_pos, k_pos, lam):
    B, Tq = q.shape[:2]
    qb = min(Q_BLOCK, Tq)
    nb = Tq // qb
    slopes = jnp.exp2(-8.0 * (jnp.arange(DIFF_HEADS, dtype=jnp.float32) + 1.0) / DIFF_HEADS)
    kf = k.astype(jnp.float32)
    vf = v.astype(jnp.float32)
    scale = DIFF_HD ** -0.5

    def block(args):
        qblk, pos = args
        s = jnp.einsum('bqhmd,bkhmd->bhmqk', qblk.astype(jnp.float32), kf) * scale
        dist = jnp.abs(pos[:, None] - k_pos[None, :]).astype(jnp.float32)
        visible = k_pos[None, :] < (pos[:, None] // CHUNK + 1) * CHUNK
        s = jnp.where(visible, s - slopes[:, None, None, None] * dist, -jnp.inf)
        p = jax.nn.softmax(s, axis=-1)
        a = p[:, :, 0] - lam * p[:, :, 1]
        return jnp.einsum('bhqk,bkhe->bqhe', a, vf)

    qs = jnp.moveaxis(q.reshape(B, nb, qb, *q.shape[2:]), 1, 0)
    ps = q_pos.reshape(nb, qb)
    o = lax.map(block, (qs, ps))
    return jnp.moveaxis(o, 0, 1).reshape(B, Tq, DIFF_HEADS, 2 * DIFF_HD).astype(v.dtype)


def diff_mixer(h, k, v, q_pos, k_pos, w_q, lam_p, g_sub, w_out, lambda_init):
    B, T, _ = h.shape
    q = (h @ w_q).reshape(B, T, DIFF_HEADS, 2, DIFF_HD)
    lp = lam_p.astype(jnp.float32)
    lam = jnp.exp(jnp.sum(lp[0] * lp[1])) - jnp.exp(jnp.sum(lp[2] * lp[3])) + lambda_init
    o = diff_attention(q, k, v, q_pos, k_pos, lam)
    o = rmsnorm(o, g_sub) * (1.0 - lambda_init)
    return o.reshape(B, T, DIFF_HEADS * 2 * DIFF_HD) @ w_out


def sq_relu_mlp(h, w_up, w_down):
    return jnp.square(jax.nn.relu(h @ w_up)) @ w_down


def trunk(x, ret_state, cache_k, cache_v, norm_mix_pre, norm_mix_post, norm_ffn_pre, norm_ffn_post,
          ret_w_in, ret_w_out, kv_norm, w_kv, diff_w_q, diff_lambda, diff_g_sub, diff_w_out,
          mlp_w_up, mlp_w_down):
    T = x.shape[1]
    past = cache_k.shape[1]
    q_pos = past + jnp.arange(T, dtype=jnp.int32)
    k_pos = jnp.arange(past + T, dtype=jnp.int32)
    new_states = []
    k_new = v_new = k_all = v_all = None
    for layer in range(DEPTH):
        h = rmsnorm(x, norm_mix_pre[layer])
        if layer < N_A_LAYERS:
            m, s = retention_mixer(h, ret_state[layer], ret_w_in[layer], ret_w_out[layer])
            new_states.append(s)
        else:
            j = layer - N_A_LAYERS
            lambda_init = 0.8 - 0.6 * math.exp(-0.3 * layer)
            m = diff_mixer(h, k_all, v_all, q_pos, k_pos, diff_w_q[j], diff_lambda[j],
                           diff_g_sub[j], diff_w_out[j], lambda_init)
        x = x + rmsnorm(m, norm_mix_post[layer])
        f = sq_relu_mlp(rmsnorm(x, norm_ffn_pre[layer]), mlp_w_up[layer], mlp_w_down[layer])
        x = x + rmsnorm(f, norm_ffn_post[layer])
        if layer == N_A_LAYERS - 1:
            k_new, v_new = shared_kv(x, kv_norm, w_kv)
            k_all = jnp.concatenate([cache_k, k_new], axis=1)
            v_all = jnp.concatenate([cache_v, v_new], axis=1)
    return x, jnp.stack(new_states), k_new, v_new


def setup_inputs(seed: int = 0) -> dict:
    key = jax.random.key(seed)
    ks = jax.random.split(key, 20)
    f32 = jnp.float32

    def w(k, shape, fan_in):
        return jax.random.normal(k, shape, f32) * (fan_in ** -0.5)

    def gain(k, shape):
        return 1.0 + 0.05 * jax.random.normal(k, shape, f32)

    return {
        'x_prompt': jax.random.normal(ks[0], (BATCH, SEQ, D_MODEL), f32),
        'x_sample': jax.random.normal(ks[1], (DEC_BATCH, DEC_SEQ, D_MODEL), f32),
        'state_ret': 0.5 * jax.random.normal(ks[2], (N_A_LAYERS, DEC_BATCH, RET_HEADS, RET_DK, RET_DV), f32),
        'cache_k': jax.random.normal(ks[3], (DEC_BATCH, PAST_LEN, DIFF_HEADS, 2, DIFF_HD), f32),
        'cache_v': jax.random.normal(ks[4], (DEC_BATCH, PAST_LEN, DIFF_HEADS, 2 * DIFF_HD), f32),
        'norm_mix_pre': gain(ks[5], (DEPTH, D_MODEL)),
        'norm_mix_post': gain(ks[6], (DEPTH, D_MODEL)),
        'norm_ffn_pre': gain(ks[7], (DEPTH, D_MODEL)),
        'norm_ffn_post': gain(ks[8], (DEPTH, D_MODEL)),
        'ret_w_in': w(ks[9], (N_A_LAYERS, D_MODEL, 6 * D_MODEL), D_MODEL),
        'ret_w_out': w(ks[10], (N_A_LAYERS, 2 * D_MODEL, D_MODEL), 2 * D_MODEL),
        'kv_norm': gain(ks[11], (D_MODEL,)),
        'w_kv': w(ks[12], (D_MODEL, 2 * D_MODEL), D_MODEL),
        'diff_w_q': w(ks[13], (N_B_LAYERS, D_MODEL, D_MODEL), D_MODEL),
        'diff_lambda': 0.1 * jax.random.normal(ks[14], (N_B_LAYERS, 4, DIFF_HD), f32),
        'diff_g_sub': gain(ks[15], (N_B_LAYERS, 2 * DIFF_HD)),
        'diff_w_out': w(ks[16], (N_B_LAYERS, D_MODEL, D_MODEL), D_MODEL),
        'mlp_w_up': w(ks[17], (DEPTH, D_MODEL, D_FF), D_MODEL),
        'mlp_w_down': w(ks[18], (DEPTH, D_FF, D_MODEL), D_FF),
    }


def reference(x_prompt, x_sample, state_ret, cache_k, cache_v, norm_mix_pre, norm_mix_post,
              norm_ffn_pre, norm_ffn_post, ret_w_in, ret_w_out, kv_norm, w_kv, diff_w_q,
              diff_lambda, diff_g_sub, diff_w_out, mlp_w_up, mlp_w_down):
    weights = (norm_mix_pre, norm_mix_post, norm_ffn_pre, norm_ffn_post, ret_w_in, ret_w_out,
               kv_norm, w_kv, diff_w_q, diff_lambda, diff_g_sub, diff_w_out, mlp_w_up, mlp_w_down)
    b = x_prompt.shape[0]
    dt = x_prompt.dtype
    s0 = jnp.zeros((N_A_LAYERS, b, RET_HEADS, RET_DK, RET_DV), dt)
    k0 = jnp.zeros((b, 0, DIFF_HEADS, 2, DIFF_HD), dt)
    v0 = jnp.zeros((b, 0, DIFF_HEADS, 2 * DIFF_HD), dt)
    y_prompt, state_ret_prompt, k_prompt, v_prompt = trunk(x_prompt, s0, k0, v0, *weights)
    y_sample, state_ret_sample, k_sample, v_sample = trunk(x_sample, state_ret, cache_k, cache_v, *weights)
    return (y_prompt, y_sample, state_ret_prompt, k_prompt, v_prompt, state_ret_sample, k_sample, v_sample)
```

```python
import functools
import math

import jax
import jax.numpy as jnp
from jax import lax
from jax.experimental import pallas as pl
from jax.experimental.pallas import tpu as pltpu

EPS = 1e-6
CHUNK = 64
RET_HEADS = 8
DIFF_HEADS = 8
BF16 = jnp.bfloat16
F32 = jnp.float32

V7X_VMEM_BYTES = 64 * 1024 * 1024
VMEM_LIMIT_BYTES = V7X_VMEM_BYTES - 8 * 1024 * 1024
LANES = 128
NEG_BIG = -1e30


def _params(*semantics):
    return pltpu.CompilerParams(dimension_semantics=semantics,
                                vmem_limit_bytes=VMEM_LIMIT_BYTES)


def _rms_scale(x):
    return lax.rsqrt(jnp.mean(x * x, axis=-1, keepdims=True) + EPS)


def _norm_matmul_kernel(x_ref, g_ref, w_ref, *rest, n_out, out_scale):
    o_refs, h_ref = rest[:n_out], rest[n_out]

    @pl.when(pl.program_id(1) == 0)
    def _():
        x = x_ref[...]
        h_ref[...] = (x * _rms_scale(x) * g_ref[...]).astype(h_ref.dtype)

    acc = jnp.dot(h_ref[...], w_ref[...], preferred_element_type=F32)
    if out_scale != 1.0:
        acc = acc * out_scale
    for o_ref in o_refs:
        o_ref[...] = acc.astype(o_ref.dtype)


def norm_matmul(x, g, w, out_dtypes, *, out_scale=1.0, tm=1024, tn=1024, name):
    m, d = x.shape
    n = w.shape[1]
    tm, tn = min(tm, m), min(tn, n)
    assert m % tm == 0 and n % tn == 0
    outs = pl.pallas_call(
        functools.partial(_norm_matmul_kernel, n_out=len(out_dtypes), out_scale=out_scale),
        grid=(m // tm, n // tn),
        in_specs=[pl.BlockSpec((tm, d), lambda i, j: (i, 0)),
                  pl.BlockSpec((1, d), lambda i, j: (0, 0)),
                  pl.BlockSpec((d, tn), lambda i, j: (0, j))],
        out_specs=[pl.BlockSpec((tm, tn), lambda i, j: (i, j)) for _ in out_dtypes],
        out_shape=[jax.ShapeDtypeStruct((m, n), dt) for dt in out_dtypes],
        scratch_shapes=[pltpu.VMEM((tm, d), BF16)],
        compiler_params=_params("parallel", "arbitrary"),
        name=name,
    )(x, g.reshape(1, d), w)
    return outs


def _matmul_norm_resid_kernel(a_ref, w_ref, g_ref, x_ref, o_ref, *, nk):
    k = pl.program_id(1)
    part = jnp.dot(a_ref[...], w_ref[...], preferred_element_type=F32)

    @pl.when(k == 0)
    def _():
        o_ref[...] = part

    @pl.when(k > 0)
    def _():
        o_ref[...] += part

    @pl.when(k == nk - 1)
    def _():
        m = o_ref[...]
        o_ref[...] = x_ref[...] + m * _rms_scale(m) * g_ref[...]


def matmul_norm_resid(a, w, g, x, *, tm=512, tk=1024, name):
    m, kdim = a.shape
    d = w.shape[1]
    tm, tk = min(tm, m), min(tk, kdim)
    assert m % tm == 0 and kdim % tk == 0
    nk = kdim // tk
    return pl.pallas_call(
        functools.partial(_matmul_norm_resid_kernel, nk=nk),
        grid=(m // tm, nk),
        in_specs=[pl.BlockSpec((tm, tk), lambda i, k: (i, k)),
                  pl.BlockSpec((tk, d), lambda i, k: (k, 0)),
                  pl.BlockSpec((1, d), lambda i, k: (0, 0)),
                  pl.BlockSpec((tm, d), lambda i, k: (i, 0))],
        out_specs=pl.BlockSpec((tm, d), lambda i, k: (i, 0)),
        out_shape=jax.ShapeDtypeStruct((m, d), F32),
        compiler_params=_params("parallel", "arbitrary"),
        name=name,
    )(a, w, g.reshape(1, d), x)


def _mlp_kernel(x_ref, gpre_ref, wu_ref, wd_ref, gpost_ref, o_ref, h_ref, *, nf):
    j = pl.program_id(1)

    @pl.when(j == 0)
    def _():
        x = x_ref[...]
        h_ref[...] = (x * _rms_scale(x) * gpre_ref[...]).astype(h_ref.dtype)

    u = jnp.maximum(jnp.dot(h_ref[...], wu_ref[...], preferred_element_type=F32), 0.0)
    part = jnp.dot((u * u).astype(BF16), wd_ref[...], preferred_element_type=F32)

    @pl.when(j == 0)
    def _():
        o_ref[...] = part

    @pl.when(j > 0)
    def _():
        o_ref[...] += part

    @pl.when(j == nf - 1)
    def _():
        f = o_ref[...]
        o_ref[...] = x_ref[...] + f * _rms_scale(f) * gpost_ref[...]


def mlp(x, g_pre, w_up, w_down, g_post, *, tm=512, tf=512, name):
    m, d = x.shape
    f = w_up.shape[1]
    tm, tf = min(tm, m), min(tf, f)
    assert m % tm == 0 and f % tf == 0
    nf = f // tf
    return pl.pallas_call(
        functools.partial(_mlp_kernel, nf=nf),
        grid=(m // tm, nf),
        in_specs=[pl.BlockSpec((tm, d), lambda i, j: (i, 0)),
                  pl.BlockSpec((1, d), lambda i, j: (0, 0)),
                  pl.BlockSpec((d, tf), lambda i, j: (0, j)),
                  pl.BlockSpec((tf, d), lambda i, j: (j, 0)),
                  pl.BlockSpec((1, d), lambda i, j: (0, 0))],
        out_specs=pl.BlockSpec((tm, d), lambda i, j: (i, 0)),
        out_shape=jax.ShapeDtypeStruct((m, d), F32),
        scratch_shapes=[pltpu.VMEM((tm, d), BF16)],
        compiler_params=_params("parallel", "arbitrary"),
        name=name,
    )(x, g_pre.reshape(1, d), w_up, w_down, g_post.reshape(1, d))


def _retention_kernel(*refs, has_s0, nc, k_scale):
    if has_s0:
        q_ref, k_ref, v_ref, g_ref, din_ref, qd_ref, kd_ref, s0_ref, o_ref, s_out_ref, s_scr = refs
    else:
        q_ref, k_ref, v_ref, g_ref, din_ref, qd_ref, kd_ref, o_ref, s_out_ref, s_scr = refs
    c = pl.program_id(2)

    @pl.when(c == 0)
    def _():
        if has_s0:
            s_scr[...] = s0_ref[...]
        else:
            s_scr[...] = jnp.zeros_like(s_scr)

    q = q_ref[...]
    k = k_ref[...].astype(F32) * k_scale
    v = v_ref[...]
    s = s_scr[...]
    q_dec = qd_ref[...]
    k_dec = kd_ref[...]
    c_dec = q_dec[q_dec.shape[0] - 1:, :]

    sc = lax.dot_general(q, k.astype(BF16), (((1,), (1,)), ((), ())),
                         preferred_element_type=F32) * din_ref[...]
    o = (jnp.dot(sc.astype(BF16), v, preferred_element_type=F32)
         + jnp.dot((q.astype(F32) * q_dec).astype(BF16), s.astype(BF16),
                   preferred_element_type=F32))
    s_new = s * c_dec + lax.dot_general((k * k_dec).astype(BF16), v, (((0,), (0,)), ((), ())),
                                        preferred_element_type=F32)
    s_scr[...] = s_new

    g = g_ref[...].astype(F32)
    o_ref[...] = (g * jax.nn.sigmoid(g) * (o * _rms_scale(o))).astype(o_ref.dtype)

    @pl.when(c == nc - 1)
    def _():
        s_out_ref[...] = s_new


def _retention_tables(chunk):
    lg = jnp.log1p(-jnp.exp2(-5.0 - jnp.arange(RET_HEADS, dtype=F32)))
    i = jnp.arange(chunk, dtype=F32)
    dif = i[:, None] - i[None, :]
    decay_in = jnp.where(dif >= 0, jnp.exp(lg[:, None, None] * jnp.maximum(dif, 0.0)), 0.0)
    q_dec = jnp.exp(lg[:, None] * (i[None, :] + 1.0))[..., None]
    k_dec = jnp.exp(lg[:, None] * (chunk - 1.0 - i[None, :]))[..., None]
    return decay_in, q_dec, k_dec


def retention(proj, s0, batch, seq, d_model, *, chunk, name):
    dk = d_model // RET_HEADS
    dv = 2 * d_model // RET_HEADS
    chunk = min(chunk, seq)
    assert seq % chunk == 0
    nc = seq // chunk
    decay_in, q_dec, k_dec = _retention_tables(chunk)
    has_s0 = s0 is not None

    row = lambda b, h, c: b * nc + c
    in_specs = [
        pl.BlockSpec((chunk, dk), lambda b, h, c: (row(b, h, c), h)),
        pl.BlockSpec((chunk, dk), lambda b, h, c: (row(b, h, c), RET_HEADS + h)),
        pl.BlockSpec((chunk, dv), lambda b, h, c: (row(b, h, c), RET_HEADS + h)),
        pl.BlockSpec((chunk, dv), lambda b, h, c: (row(b, h, c), 2 * RET_HEADS + h)),
        pl.BlockSpec((None, chunk, chunk), lambda b, h, c: (h, 0, 0)),
        pl.BlockSpec((None, chunk, 1), lambda b, h, c: (h, 0, 0)),
        pl.BlockSpec((None, chunk, 1), lambda b, h, c: (h, 0, 0)),
    ]
    args = [proj, proj, proj, proj, decay_in, q_dec, k_dec]
    if has_s0:
        in_specs.append(pl.BlockSpec((None, None, dk, dv), lambda b, h, c: (b, h, 0, 0)))
        args.append(s0)
    gated, s_new = pl.pallas_call(
        functools.partial(_retention_kernel, has_s0=has_s0, nc=nc, k_scale=dk ** -0.5),
        grid=(batch, RET_HEADS, nc),
        in_specs=in_specs,
        out_specs=[pl.BlockSpec((chunk, dv), lambda b, h, c: (row(b, h, c), h)),
                   pl.BlockSpec((None, None, dk, dv), lambda b, h, c: (b, h, 0, 0))],
        out_shape=[jax.ShapeDtypeStruct((batch * seq, RET_HEADS * dv), BF16),
                   jax.ShapeDtypeStruct((batch, RET_HEADS, dk, dv), F32)],
        scratch_shapes=[pltpu.VMEM((dk, dv), F32)],
        compiler_params=_params("parallel", "parallel", "arbitrary"),
        name=name,
    )(*args)
    return gated, s_new


def _diff_attn_kernel(slopes_ref, lam_ref, q_ref, k_ref, v_ref, gsub_ref, o_ref, *,
                      bq, bk, nk, past, tk_true, hd, lambda_init):
    h = pl.program_id(1)
    i = pl.program_id(2)
    slope = slopes_ref[h]

    q = q_ref[...]
    q_maps = (q[:, :hd], q[:, hd:])
    row_start = past + i * bq
    row_pos = row_start + lax.broadcasted_iota(jnp.int32, (bq, bk), 0)
    row_lim = jnp.minimum((row_pos // CHUNK + 1) * CHUNK, tk_true)
    col_iota = lax.broadcasted_iota(jnp.int32, (bq, bk), 1)

    first_lim = jnp.minimum((row_start // CHUNK + 1) * CHUNK, tk_true)
    last_lim = jnp.minimum(((row_start + bq - 1) // CHUNK + 1) * CHUNK, tk_true)
    n_full = jnp.minimum(first_lim // bk, nk)
    n_vis = jnp.minimum((last_lim + bk - 1) // bk, nk)

    def step(j, carry, masked):
        start = pl.multiple_of(j * bk, bk)
        k_blk = k_ref[pl.ds(start, bk), :]
        v_blk = v_ref[pl.ds(start, bk), :]
        col_pos = start + col_iota
        bias = -slope * jnp.abs(row_pos - col_pos).astype(F32)
        if masked:
            bias = jnp.where(col_pos < row_lim, bias, NEG_BIG)
        new = []
        for mi in range(2):
            m_old, l_old, a_old = carry[mi]
            s = lax.dot_general(q_maps[mi], k_blk[:, mi * hd:(mi + 1) * hd],
                                (((1,), (1,)), ((), ())), preferred_element_type=F32) + bias
            m_new = jnp.maximum(m_old, jnp.max(s, axis=-1, keepdims=True))
            alpha = jnp.exp(m_old - m_new)
            p = jnp.exp(s - m_new)
            l_new = alpha * l_old + jnp.sum(p, axis=-1, keepdims=True)
            a_new = alpha * a_old + jnp.dot(p.astype(BF16), v_blk, preferred_element_type=F32)
            new.append((m_new, l_new, a_new))
        return tuple(new)

    init = tuple((jnp.full((bq, 1), NEG_BIG, F32), jnp.zeros((bq, 1), F32),
                  jnp.zeros((bq, 2 * hd), F32)) for _ in range(2))
    carry = lax.fori_loop(0, n_full, functools.partial(step, masked=False), init)
    carry = lax.fori_loop(n_full, n_vis, functools.partial(step, masked=True), carry)
    (_, l0, a0), (_, l1, a1) = carry

    lp = lam_ref[...]
    lam = (jnp.exp(jnp.sum(lp[0:1] * lp[1:2], axis=-1, keepdims=True))
           - jnp.exp(jnp.sum(lp[2:3] * lp[3:4], axis=-1, keepdims=True)) + lambda_init)
    o = a0 / l0 - lam * (a1 / l1)
    o_ref[...] = (o * _rms_scale(o) * gsub_ref[...] * (1.0 - lambda_init)).astype(o_ref.dtype)


def diff_attention(q, k_all, v_all, lam_p, g_sub, batch, seq, tk_pad, tk_true, past,
                   lambda_init, *, bq, bk, name):
    d = q.shape[1]
    hd = d // (2 * DIFF_HEADS)
    bq, bk = min(bq, seq), min(bk, tk_pad)
    assert seq % bq == 0 and tk_pad % bk == 0
    nq, nk = seq // bq, tk_pad // bk
    slopes = jnp.exp2(-8.0 * (jnp.arange(DIFF_HEADS, dtype=F32) + 1.0) / DIFF_HEADS)
    return pl.pallas_call(
        functools.partial(_diff_attn_kernel, bq=bq, bk=bk, nk=nk, past=past, tk_true=tk_true,
                          hd=hd, lambda_init=lambda_init),
        grid=(batch, DIFF_HEADS, nq),
        in_specs=[pl.BlockSpec(memory_space=pltpu.SMEM),
                  pl.BlockSpec((4, hd), lambda b, h, i: (0, 0)),
                  pl.BlockSpec((bq, 2 * hd), lambda b, h, i: (b * nq + i, h)),
                  pl.BlockSpec((tk_pad, 2 * hd), lambda b, h, i: (b, h)),
                  pl.BlockSpec((tk_pad, 2 * hd), lambda b, h, i: (b, h)),
                  pl.BlockSpec((1, 2 * hd), lambda b, h, i: (0, 0))],
        out_specs=pl.BlockSpec((bq, 2 * hd), lambda b, h, i: (b * nq + i, h)),
        out_shape=jax.ShapeDtypeStruct((batch * seq, d), BF16),
        compiler_params=_params("parallel", "parallel", "arbitrary"),
        name=name,
    )(slopes, lam_p, q, k_all, v_all, g_sub.reshape(1, 2 * hd))


def _round_up(n, mult):
    return (n + mult - 1) // mult * mult


def _trunk(tag, x, ret_state, cache_k, cache_v, w):
    batch, seq, d = x.shape
    m = batch * seq
    past = cache_k.shape[1] if cache_k is not None else 0
    depth = w["norm_mix_pre"].shape[0]
    n_a = w["ret_w_in"].shape[0]
    hd = d // (2 * DIFF_HEADS)
    tk_true = past + seq

    xf = x.reshape(m, d)
    states = []
    k_new = v_new = k_all = v_all = None
    tk_pad = tk_true
    for layer in range(depth):
        nm = f"{tag}_l{layer}"
        if layer < n_a:
            (proj,) = norm_matmul(xf, w["norm_mix_pre"][layer], w["ret_w_in"][layer], [BF16],
                                  name=f"{nm}_ret_in")
            s0 = None if ret_state is None else ret_state[layer]
            gated, s_new = retention(proj, s0, batch, seq, d, chunk=256, name=f"{nm}_retention")
            states.append(s_new)
            xf = matmul_norm_resid(gated, w["ret_w_out"][layer], w["norm_mix_post"][layer], xf,
                                   name=f"{nm}_ret_out")
        else:
            j = layer - n_a
            lambda_init = 0.8 - 0.6 * math.exp(-0.3 * layer)
            (q,) = norm_matmul(xf, w["norm_mix_pre"][layer], w["diff_w_q"][j], [BF16],
                               out_scale=hd ** -0.5, name=f"{nm}_diff_q")
            o = diff_attention(q, k_all, v_all, w["diff_lambda"][j], w["diff_g_sub"][j],
                               batch, seq, tk_pad, tk_true, past, lambda_init,
                               bq=256, bk=256 if past == 0 else LANES, name=f"{nm}_diff_attn")
            xf = matmul_norm_resid(o, w["diff_w_out"][j], w["norm_mix_post"][layer], xf,
                                   name=f"{nm}_diff_out")
        xf = mlp(xf, w["norm_ffn_pre"][layer], w["mlp_w_up"][layer], w["mlp_w_down"][layer],
                 w["norm_ffn_post"][layer], name=f"{nm}_mlp")
        if layer == n_a - 1:
            k_new, k_b = norm_matmul(xf, w["kv_norm"], w["w_k"], [F32, BF16], name=f"{tag}_shared_k")
            v_new, v_b = norm_matmul(xf, w["kv_norm"], w["w_v"], [F32, BF16], name=f"{tag}_shared_v")
            if past == 0:
                k_all, v_all = k_b, v_b
            else:
                tk_pad = _round_up(tk_true, LANES)
                pad = jnp.zeros((batch, tk_pad - tk_true, d), BF16)
                k_all = jnp.concatenate([cache_k.reshape(batch, past, d).astype(BF16),
                                         k_b.reshape(batch, seq, d), pad], axis=1).reshape(batch * tk_pad, d)
                v_all = jnp.concatenate([cache_v.reshape(batch, past, d).astype(BF16),
                                         v_b.reshape(batch, seq, d), pad], axis=1).reshape(batch * tk_pad, d)
    y = xf.reshape(batch, seq, d)
    return (y, jnp.stack(states),
            k_new.reshape(batch, seq, DIFF_HEADS, 2, hd), v_new.reshape(batch, seq, DIFF_HEADS, 2 * hd))


def kernel(x_prompt, x_sample, state_ret, cache_k, cache_v, norm_mix_pre, norm_mix_post,
           norm_ffn_pre, norm_ffn_post, ret_w_in, ret_w_out, kv_norm, w_kv, diff_w_q,
           diff_lambda, diff_g_sub, diff_w_out, mlp_w_up, mlp_w_down):
    d = x_prompt.shape[-1]
    w = dict(norm_mix_pre=norm_mix_pre, norm_mix_post=norm_mix_post, norm_ffn_pre=norm_ffn_pre,
             norm_ffn_post=norm_ffn_post, kv_norm=kv_norm, diff_lambda=diff_lambda,
             diff_g_sub=diff_g_sub,
             ret_w_in=ret_w_in.astype(BF16), ret_w_out=ret_w_out.astype(BF16),
             w_k=w_kv[:, :d].astype(BF16), w_v=w_kv[:, d:].astype(BF16),
             diff_w_q=diff_w_q.astype(BF16), diff_w_out=diff_w_out.astype(BF16),
             mlp_w_up=mlp_w_up.astype(BF16), mlp_w_down=mlp_w_down.astype(BF16))
    y_p, s_p, k_p, v_p = _trunk("prompt", x_prompt, None, None, None, w)
    y_s, s_s, k_s, v_s = _trunk("sample", x_sample, state_ret, cache_k, cache_v, w)
    return (y_p, y_s, s_p, k_p, v_p, s_s, k_s, v_s)
```

```python
import functools
import math

import jax
import jax.numpy as jnp
from jax import lax
from jax.experimental import pallas as pl
from jax.experimental.pallas import tpu as pltpu

EPS = 1e-6
CHUNK = 64
RET_HEADS = 8
DIFF_HEADS = 8
BF16 = jnp.bfloat16
F32 = jnp.float32

V7X_VMEM_BYTES = 64 * 1024 * 1024
VMEM_LIMIT_BYTES = V7X_VMEM_BYTES - 8 * 1024 * 1024
LANES = 128
NEG_BIG = -1e30
LOG2E = math.log2(math.e)


def _params(*semantics):
    return pltpu.CompilerParams(dimension_semantics=semantics,
                                vmem_limit_bytes=VMEM_LIMIT_BYTES)


def _rms_scale(x, axis=-1):
    return lax.rsqrt(jnp.mean(x * x, axis=axis, keepdims=True) + EPS)


def _norm_matmul_kernel(x_ref, g_ref, w_ref, *rest, n_out, out_scale):
    o_refs, h_ref = rest[:n_out], rest[n_out]

    @pl.when(pl.program_id(1) == 0)
    def _():
        x = x_ref[...]
        h_ref[...] = (x * _rms_scale(x) * g_ref[...]).astype(h_ref.dtype)

    acc = jnp.dot(h_ref[...], w_ref[...], preferred_element_type=F32)
    if out_scale != 1.0:
        acc = acc * out_scale
    for o_ref in o_refs:
        o_ref[...] = acc.astype(o_ref.dtype)


def norm_matmul(x, g, w, out_dtypes, *, out_scale=1.0, tm=1024, tn=1024, name):
    m, d = x.shape
    n = w.shape[1]
    tm, tn = min(tm, m), min(tn, n)
    assert m % tm == 0 and n % tn == 0
    outs = pl.pallas_call(
        functools.partial(_norm_matmul_kernel, n_out=len(out_dtypes), out_scale=out_scale),
        grid=(m // tm, n // tn),
        in_specs=[pl.BlockSpec((tm, d), lambda i, j: (i, 0)),
                  pl.BlockSpec((1, d), lambda i, j: (0, 0)),
                  pl.BlockSpec((d, tn), lambda i, j: (0, j))],
        out_specs=[pl.BlockSpec((tm, tn), lambda i, j: (i, j)) for _ in out_dtypes],
        out_shape=[jax.ShapeDtypeStruct((m, n), dt) for dt in out_dtypes],
        scratch_shapes=[pltpu.VMEM((tm, d), BF16)],
        compiler_params=_params("parallel", "arbitrary"),
        name=name,
    )(x, g.reshape(1, d), w)
    return outs


def _matmul_norm_resid_kernel(a_ref, w_ref, g_ref, x_ref, o_ref, *, nk):
    k = pl.program_id(1)

    @pl.when(k == 0)
    def _():
        o_ref[...] = jnp.zeros_like(o_ref)

    o_ref[...] += jnp.dot(a_ref[...], w_ref[...], preferred_element_type=F32)

    @pl.when(k == nk - 1)
    def _():
        m = o_ref[...]
        o_ref[...] = x_ref[...] + m * _rms_scale(m) * g_ref[...]


def matmul_norm_resid(a, w, g, x, *, tm=512, tk=1024, name):
    m, kdim = a.shape
    d = w.shape[1]
    tm, tk = min(tm, m), min(tk, kdim)
    assert m % tm == 0 and kdim % tk == 0
    nk = kdim // tk
    return pl.pallas_call(
        functools.partial(_matmul_norm_resid_kernel, nk=nk),
        grid=(m // tm, nk),
        in_specs=[pl.BlockSpec((tm, tk), lambda i, k: (i, k)),
                  pl.BlockSpec((tk, d), lambda i, k: (k, 0)),
                  pl.BlockSpec((1, d), lambda i, k: (0, 0)),
                  pl.BlockSpec((tm, d), lambda i, k: (i, 0))],
        out_specs=pl.BlockSpec((tm, d), lambda i, k: (i, 0)),
        out_shape=jax.ShapeDtypeStruct((m, d), F32),
        compiler_params=_params("parallel", "arbitrary"),
        name=name,
    )(a, w, g.reshape(1, d), x)


def _mlp_kernel(x_ref, gpre_ref, wu_ref, wd_ref, gpost_ref, o_ref, h_ref, *, nf):
    j = pl.program_id(1)

    @pl.when(j == 0)
    def _():
        x = x_ref[...]
        h_ref[...] = (x * _rms_scale(x) * gpre_ref[...]).astype(h_ref.dtype)
        o_ref[...] = jnp.zeros_like(o_ref)

    u = jnp.maximum(jnp.dot(h_ref[...], wu_ref[...], preferred_element_type=F32), 0.0)
    o_ref[...] += jnp.dot((u * u).astype(BF16), wd_ref[...], preferred_element_type=F32)

    @pl.when(j == nf - 1)
    def _():
        f = o_ref[...]
        o_ref[...] = x_ref[...] + f * _rms_scale(f) * gpost_ref[...]


def mlp(x, g_pre, w_up, w_down, g_post, *, tm=1024, tf=512, name):
    m, d = x.shape
    f = w_up.shape[1]
    tm, tf = min(tm, m), min(tf, f)
    assert m % tm == 0 and f % tf == 0
    nf = f // tf
    return pl.pallas_call(
        functools.partial(_mlp_kernel, nf=nf),
        grid=(m // tm, nf),
        in_specs=[pl.BlockSpec((tm, d), lambda i, j: (i, 0)),
                  pl.BlockSpec((1, d), lambda i, j: (0, 0)),
                  pl.BlockSpec((d, tf), lambda i, j: (0, j)),
                  pl.BlockSpec((tf, d), lambda i, j: (j, 0)),
                  pl.BlockSpec((1, d), lambda i, j: (0, 0))],
        out_specs=pl.BlockSpec((tm, d), lambda i, j: (i, 0)),
        out_shape=jax.ShapeDtypeStruct((m, d), F32),
        scratch_shapes=[pltpu.VMEM((tm, d), BF16)],
        compiler_params=_params("parallel", "arbitrary"),
        name=name,
    )(x, g_pre.reshape(1, d), w_up, w_down, g_post.reshape(1, d))


def _retention_kernel(*refs, has_s0, nc, k_scale):
    if has_s0:
        q_ref, k_ref, v_ref, g_ref, din_ref, qd_ref, kd_ref, s0_ref, o_ref, s_out_ref, s_scr = refs
    else:
        q_ref, k_ref, v_ref, g_ref, din_ref, qd_ref, kd_ref, o_ref, s_out_ref, s_scr = refs
    c = pl.program_id(2)

    @pl.when(c == 0)
    def _():
        if has_s0:
            s_scr[...] = s0_ref[...]
        else:
            s_scr[...] = jnp.zeros_like(s_scr)

    q = q_ref[...]
    k = k_ref[...].astype(F32) * k_scale
    v = v_ref[...]
    s = s_scr[...]
    q_dec = qd_ref[...]
    k_dec = kd_ref[...]
    c_dec = q_dec[q_dec.shape[0] - 1:, :]

    sc = lax.dot_general(q, k.astype(BF16), (((1,), (1,)), ((), ())),
                         preferred_element_type=F32) * din_ref[...]
    o = (jnp.dot(sc.astype(BF16), v, preferred_element_type=F32)
         + jnp.dot((q.astype(F32) * q_dec).astype(BF16), s.astype(BF16),
                   preferred_element_type=F32))
    s_new = s * c_dec + lax.dot_general((k * k_dec).astype(BF16), v, (((0,), (0,)), ((), ())),
                                        preferred_element_type=F32)
    s_scr[...] = s_new

    g = g_ref[...].astype(F32)
    o_ref[...] = (g * jax.nn.sigmoid(g) * (o * _rms_scale(o))).astype(o_ref.dtype)

    @pl.when(c == nc - 1)
    def _():
        s_out_ref[...] = s_new


def _retention_tables(chunk):
    lg = jnp.log1p(-jnp.exp2(-5.0 - jnp.arange(RET_HEADS, dtype=F32)))
    i = jnp.arange(chunk, dtype=F32)
    dif = i[:, None] - i[None, :]
    decay_in = jnp.where(dif >= 0, jnp.exp(lg[:, None, None] * jnp.maximum(dif, 0.0)), 0.0)
    q_dec = jnp.exp(lg[:, None] * (i[None, :] + 1.0))[..., None]
    k_dec = jnp.exp(lg[:, None] * (chunk - 1.0 - i[None, :]))[..., None]
    return decay_in, q_dec, k_dec


def retention(proj, s0, batch, seq, d_model, *, chunk, name):
    dk = d_model // RET_HEADS
    dv = 2 * d_model // RET_HEADS
    chunk = min(chunk, seq)
    assert seq % chunk == 0
    nc = seq // chunk
    decay_in, q_dec, k_dec = _retention_tables(chunk)
    has_s0 = s0 is not None

    row = lambda b, h, c: b * nc + c
    in_specs = [
        pl.BlockSpec((chunk, dk), lambda b, h, c: (row(b, h, c), h)),
        pl.BlockSpec((chunk, dk), lambda b, h, c: (row(b, h, c), RET_HEADS + h)),
        pl.BlockSpec((chunk, dv), lambda b, h, c: (row(b, h, c), RET_HEADS + h)),
        pl.BlockSpec((chunk, dv), lambda b, h, c: (row(b, h, c), 2 * RET_HEADS + h)),
        pl.BlockSpec((None, chunk, chunk), lambda b, h, c: (h, 0, 0)),
        pl.BlockSpec((None, chunk, 1), lambda b, h, c: (h, 0, 0)),
        pl.BlockSpec((None, chunk, 1), lambda b, h, c: (h, 0, 0)),
    ]
    args = [proj, proj, proj, proj, decay_in, q_dec, k_dec]
    if has_s0:
        in_specs.append(pl.BlockSpec((None, None, dk, dv), lambda b, h, c: (b, h, 0, 0)))
        args.append(s0)
    gated, s_new = pl.pallas_call(
        functools.partial(_retention_kernel, has_s0=has_s0, nc=nc, k_scale=dk ** -0.5),
        grid=(batch, RET_HEADS, nc),
        in_specs=in_specs,
        out_specs=[pl.BlockSpec((chunk, dv), lambda b, h, c: (row(b, h, c), h)),
                   pl.BlockSpec((None, None, dk, dv), lambda b, h, c: (b, h, 0, 0))],
        out_shape=[jax.ShapeDtypeStruct((batch * seq, RET_HEADS * dv), BF16),
                   jax.ShapeDtypeStruct((batch, RET_HEADS, dk, dv), F32)],
        scratch_shapes=[pltpu.VMEM((dk, dv), F32)],
        compiler_params=_params("parallel", "parallel", "arbitrary"),
        name=name,
    )(*args)
    return gated, s_new


def _diff_lambda(lam_ref, lambda_init):
    lp = lam_ref[...]
    return (jnp.exp(jnp.sum(lp[0:1] * lp[1:2], axis=-1, keepdims=True))
            - jnp.exp(jnp.sum(lp[2:3] * lp[3:4], axis=-1, keepdims=True)) + lambda_init)


def _alibi_slopes():
    return jnp.exp2(-8.0 * (jnp.arange(DIFF_HEADS, dtype=F32) + 1.0) / DIFF_HEADS)


def _diff_attn_long_kernel(shift_ref, lam_ref, qt_ref, k_ref, vt_ref, boff_ref, bdiag_ref, gsub_ref,
                           o_ref, m_scr, l_scr, acc_scr, *, blk, hd, lambda_init):
    h = pl.program_id(1)
    i = pl.program_id(2)
    shift = shift_ref[h]

    m_scr[...] = jnp.full_like(m_scr, NEG_BIG)
    l_scr[...] = jnp.zeros_like(l_scr)
    acc_scr[...] = jnp.zeros_like(acc_scr)

    def tile(j, bias_ref, offset):
        start = pl.multiple_of(j * blk, blk)
        k_blk = k_ref[pl.ds(start, blk), :]
        vt = vt_ref[j]
        sts = [jnp.dot(k_blk[:, mi * hd:(mi + 1) * hd], qt_ref[mi * hd:(mi + 1) * hd, :],
                       preferred_element_type=F32) + bias_ref[...] for mi in range(2)]
        for mi in range(2):
            m_old = m_scr[mi]
            m_new = jnp.maximum(m_old, jnp.max(sts[mi], axis=0, keepdims=True) - offset)
            alpha = jnp.exp2(m_old - m_new)
            pt = jnp.exp2(sts[mi] - (m_new + offset))
            l_scr[mi] = alpha * l_scr[mi] + jnp.sum(pt, axis=0, keepdims=True)
            acc_scr[mi] = alpha * acc_scr[mi] + jnp.dot(vt, pt.astype(BF16),
                                                        preferred_element_type=F32)
            m_scr[mi] = m_new

    def off_diagonal(j, carry):
        tile(j, boff_ref, shift * (i - j).astype(F32))
        return carry

    lax.fori_loop(0, i, off_diagonal, 0)
    tile(i, bdiag_ref, 0.0)

    lam = _diff_lambda(lam_ref, lambda_init)
    ot = acc_scr[0] * (1.0 / l_scr[0]) - lam * (acc_scr[1] * (1.0 / l_scr[1]))
    ot = ot * _rms_scale(ot, axis=0) * gsub_ref[...] * (1.0 - lambda_init)
    o_ref[...] = ot.T.astype(o_ref.dtype)


def diff_attention_long(q, k, v, lam_p, g_sub, batch, seq, lambda_init, *, blk, name):
    d = q.shape[1]
    hd = d // (2 * DIFF_HEADS)
    assert seq % blk == 0 and blk % CHUNK == 0
    nq = seq // blk
    slopes = _alibi_slopes() * LOG2E
    r = jnp.arange(blk, dtype=jnp.int32)
    rel = (r[None, :] - r[:, None])
    visible = r[:, None] < (r[None, :] // CHUNK + 1) * CHUNK
    b_off = -slopes[:, None, None] * rel.astype(F32)
    b_diag = jnp.where(visible, -slopes[:, None, None] * jnp.abs(rel).astype(F32), NEG_BIG)
    q_t = q.T
    v_t = v.reshape(batch, nq, blk, DIFF_HEADS, 2 * hd).transpose(0, 3, 1, 4, 2)
    return pl.pallas_call(
        functools.partial(_diff_attn_long_kernel, blk=blk, hd=hd, lambda_init=lambda_init),
        grid=(batch, DIFF_HEADS, nq),
        in_specs=[pl.BlockSpec(memory_space=pltpu.SMEM),
                  pl.BlockSpec((4, hd), lambda b, h, i: (0, 0)),
                  pl.BlockSpec((2 * hd, blk), lambda b, h, i: (h, b * nq + i)),
                  pl.BlockSpec((seq, 2 * hd), lambda b, h, i: (b, h)),
                  pl.BlockSpec((None, None, nq, 2 * hd, blk), lambda b, h, i: (b, h, 0, 0, 0)),
                  pl.BlockSpec((None, blk, blk), lambda b, h, i: (h, 0, 0)),
                  pl.BlockSpec((None, blk, blk), lambda b, h, i: (h, 0, 0)),
                  pl.BlockSpec((2 * hd, 1), lambda b, h, i: (0, 0))],
        out_specs=pl.BlockSpec((blk, 2 * hd), lambda b, h, i: (b * nq + i, h)),
        out_shape=jax.ShapeDtypeStruct((batch * seq, d), BF16),
        scratch_shapes=[pltpu.VMEM((2, 1, blk), F32), pltpu.VMEM((2, 1, blk), F32),
                        pltpu.VMEM((2, 2 * hd, blk), F32)],
        compiler_params=_params("parallel", "parallel", "arbitrary"),
        name=name,
    )(slopes * blk, lam_p, q_t, k, v_t, b_off, b_diag, g_sub.reshape(2 * hd, 1))


def _diff_attn_short_kernel(slopes_ref, lam_ref, q_ref, k_ref, v_ref, gsub_ref, o_ref, *,
                            past, tk_true, hd, lambda_init):
    tq, tk = q_ref.shape[0], k_ref.shape[0]
    row_pos = past + lax.broadcasted_iota(jnp.int32, (tq, tk), 0)
    col_pos = lax.broadcasted_iota(jnp.int32, (tq, tk), 1)
    visible = col_pos < jnp.minimum((row_pos // CHUNK + 1) * CHUNK, tk_true)
    dist = jnp.abs(row_pos - col_pos).astype(F32)
    lam = _diff_lambda(lam_ref, lambda_init)
    gsub = gsub_ref[...] * (1.0 - lambda_init)
    for h in range(DIFF_HEADS):
        bias = jnp.where(visible, -slopes_ref[h] * dist, NEG_BIG)
        cols = slice(h * 2 * hd, (h + 1) * 2 * hd)
        q, k, v = q_ref[:, cols], k_ref[:, cols], v_ref[:, cols]
        maps = []
        for mi in range(2):
            s = lax.dot_general(q[:, mi * hd:(mi + 1) * hd], k[:, mi * hd:(mi + 1) * hd],
                                (((1,), (1,)), ((), ())), preferred_element_type=F32) + bias
            p = jnp.exp2(s - jnp.max(s, axis=-1, keepdims=True))
            a = jnp.dot(p.astype(BF16), v, preferred_element_type=F32)
            maps.append(a * (1.0 / jnp.sum(p, axis=-1, keepdims=True)))
        o = maps[0] - lam * maps[1]
        o_ref[:, cols] = (o * _rms_scale(o) * gsub).astype(o_ref.dtype)


def diff_attention_short(q, k_all, v_all, lam_p, g_sub, batch, seq, tk_pad, tk_true, past,
                         lambda_init, *, name):
    d = q.shape[1]
    hd = d // (2 * DIFF_HEADS)
    return pl.pallas_call(
        functools.partial(_diff_attn_short_kernel, past=past, tk_true=tk_true, hd=hd,
                          lambda_init=lambda_init),
        grid=(batch,),
        in_specs=[pl.BlockSpec(memory_space=pltpu.SMEM),
                  pl.BlockSpec((4, hd), lambda b: (0, 0)),
                  pl.BlockSpec((seq, d), lambda b: (b, 0)),
                  pl.BlockSpec((tk_pad, d), lambda b: (b, 0)),
                  pl.BlockSpec((tk_pad, d), lambda b: (b, 0)),
                  pl.BlockSpec((1, 2 * hd), lambda b: (0, 0))],
        out_specs=pl.BlockSpec((seq, d), lambda b: (b, 0)),
        out_shape=jax.ShapeDtypeStruct((batch * seq, d), BF16),
        compiler_params=_params("parallel"),
        name=name,
    )(_alibi_slopes() * LOG2E, lam_p, q, k_all, v_all, g_sub.reshape(1, 2 * hd))


def _round_up(n, mult):
    return (n + mult - 1) // mult * mult


def _trunk(tag, x, ret_state, cache_k, cache_v, w):
    batch, seq, d = x.shape
    m = batch * seq
    past = cache_k.shape[1] if cache_k is not None else 0
    depth = w["norm_mix_pre"].shape[0]
    n_a = w["ret_w_in"].shape[0]
    hd = d // (2 * DIFF_HEADS)
    tk_true = past + seq

    xf = x.reshape(m, d)
    states = []
    k_new = v_new = k_all = v_all = None
    tk_pad = tk_true
    for layer in range(depth):
        nm = f"{tag}_l{layer}"
        if layer < n_a:
            (proj,) = norm_matmul(xf, w["norm_mix_pre"][layer], w["ret_w_in"][layer], [BF16],
                                  name=f"{nm}_ret_in")
            s0 = None if ret_state is None else ret_state[layer]
            gated, s_new = retention(proj, s0, batch, seq, d, chunk=256, name=f"{nm}_retention")
            states.append(s_new)
            xf = matmul_norm_resid(gated, w["ret_w_out"][layer], w["norm_mix_post"][layer], xf,
                                   name=f"{nm}_ret_out")
        else:
            j = layer - n_a
            lambda_init = 0.8 - 0.6 * math.exp(-0.3 * layer)
            (q,) = norm_matmul(xf, w["norm_mix_pre"][layer], w["diff_w_q"][j], [BF16],
                               out_scale=hd ** -0.5 * LOG2E, name=f"{nm}_diff_q")
            if past == 0:
                o = diff_attention_long(q, k_all, v_all, w["diff_lambda"][j], w["diff_g_sub"][j],
                                        batch, seq, lambda_init, blk=512, name=f"{nm}_diff_attn")
            else:
                o = diff_attention_short(q, k_all, v_all, w["diff_lambda"][j], w["diff_g_sub"][j],
                                         batch, seq, tk_pad, tk_true, past, lambda_init,
                                         name=f"{nm}_diff_attn")
            xf = matmul_norm_resid(o, w["diff_w_out"][j], w["norm_mix_post"][layer], xf,
                                   name=f"{nm}_diff_out")
        xf = mlp(xf, w["norm_ffn_pre"][layer], w["mlp_w_up"][layer], w["mlp_w_down"][layer],
                 w["norm_ffn_post"][layer], name=f"{nm}_mlp")
        if layer == n_a - 1:
            k_new, k_b = norm_matmul(xf, w["kv_norm"], w["w_k"], [F32, BF16], name=f"{tag}_shared_k")
            v_new, v_b = norm_matmul(xf, w["kv_norm"], w["w_v"], [F32, BF16], name=f"{tag}_shared_v")
            if past == 0:
                k_all, v_all = k_b, v_b
            else:
                tk_pad = _round_up(tk_true, LANES)
                pad = jnp.zeros((batch, tk_pad - tk_true, d), BF16)
                k_all = jnp.concatenate([cache_k.reshape(batch, past, d).astype(BF16),
                                         k_b.reshape(batch, seq, d), pad], axis=1).reshape(batch * tk_pad, d)
                v_all = jnp.concatenate([cache_v.reshape(batch, past, d).astype(BF16),
                                         v_b.reshape(batch, seq, d), pad], axis=1).reshape(batch * tk_pad, d)
    y = xf.reshape(batch, seq, d)
    return (y, jnp.stack(states),
            k_new.reshape(batch, seq, DIFF_HEADS, 2, hd), v_new.reshape(batch, seq, DIFF_HEADS, 2 * hd))


def kernel(x_prompt, x_sample, state_ret, cache_k, cache_v, norm_mix_pre, norm_mix_post,
           norm_ffn_pre, norm_ffn_post, ret_w_in, ret_w_out, kv_norm, w_kv, diff_w_q,
           diff_lambda, diff_g_sub, diff_w_out, mlp_w_up, mlp_w_down):
    d = x_prompt.shape[-1]
    w = dict(norm_mix_pre=norm_mix_pre, norm_mix_post=norm_mix_post, norm_ffn_pre=norm_ffn_pre,
             norm_ffn_post=norm_ffn_post, kv_norm=kv_norm, diff_lambda=diff_lambda,
             diff_g_sub=diff_g_sub,
             ret_w_in=ret_w_in.astype(BF16), ret_w_out=ret_w_out.astype(BF16),
             w_k=w_kv[:, :d].astype(BF16), w_v=w_kv[:, d:].astype(BF16),
             diff_w_q=diff_w_q.astype(BF16), diff_w_out=diff_w_out.astype(BF16),
             mlp_w_up=mlp_w_up.astype(BF16), mlp_w_down=mlp_w_down.astype(BF16))
    y_p, s_p, k_p, v_p = _trunk("prompt", x_prompt, None, None, None, w)
    y_s, s_s, k_s, v_s = _trunk("sample", x_sample, state_ret, cache_k, cache_v, w)
    return (y_p, y_s, s_p, k_p, v_p, s_s, k_s, v_s)
```
